```python
import math
import jax, jax.numpy as jnp
from jax import lax
import numpy as np

D_MODEL = 1024
BATCH = 4
SEQ = 8192
DEPTH = 4

CHUNK = 64
Q_BLOCK = 128
EPS = 1e-6
NEG_INF = -1e30

GM_WIDTH = 512
GM_GROUPS = 4
GM_CH = GM_WIDTH // GM_GROUPS
GM_SPAN = 128

MLA_HEADS = 4
MLA_Q_RANK = 256
MLA_KV_RANK = 128
MLA_NOPE = 128
MLA_ROPE = 64
MLA_V = 128
ROPE_THETA = 10000.0

DIFF_HEADS = 4
DIFF_HEAD = 64
DIFF_V = 2 * DIFF_HEAD

REL_BUCKETS = 32
REL_MAX_DIST = 128

N_BRANCH = 3
BRANCH_WIDTH = 512

FFN_DENSE = 2816
N_EXPERTS = 8
TOP_K = 2
FFN_EXPERT = 3584
N_DENSE = (DEPTH + 1) // 2
N_MOE = DEPTH // 2

A_COLS = 2 * GM_WIDTH
C_Q_COLS = DIFF_HEADS * 2 * DIFF_HEAD
C_K_COLS = DIFF_HEADS * 2 * DIFF_HEAD
C_V_COLS = DIFF_HEADS * DIFF_V
IN_SIZES = (A_COLS, MLA_Q_RANK, MLA_KV_RANK, MLA_ROPE, C_Q_COLS, C_K_COLS, C_V_COLS)
IN_SPLITS = tuple(int(v) for v in np.cumsum(IN_SIZES)[:-1])
IN_WIDTH = int(sum(IN_SIZES))

kernel_name = 'hybrid_gmlp_mla_diffattn_moe_encoder'


def rmsnorm(x, g):
    xf = x.astype(jnp.float32)
    y = xf * lax.rsqrt(jnp.mean(xf * xf, axis=-1, keepdims=True) + EPS)
    return (y * g.astype(jnp.float32)).astype(x.dtype)


def layernorm(x, g, b):
    xf = x.astype(jnp.float32)
    mu = jnp.mean(xf, axis=-1, keepdims=True)
    xc = xf - mu
    y = xc * lax.rsqrt(jnp.mean(xc * xc, axis=-1, keepdims=True) + EPS)
    return (y * g.astype(jnp.float32) + b.astype(jnp.float32)).astype(x.dtype)


def rope_tables(seq):
    pos = jnp.arange(seq, dtype=jnp.float32)
    inv = ROPE_THETA ** (-jnp.arange(0, MLA_ROPE, 2, dtype=jnp.float32) / MLA_ROPE)
    ang = pos[:, None] * inv[None, :]
    return jnp.cos(ang), jnp.sin(ang)


def rope(x, cos, sin):
    half = x.shape[-1] // 2
    x1, x2 = x[..., :half], x[..., half:]
    return jnp.concatenate([x1 * cos - x2 * sin, x2 * cos + x1 * sin], axis=-1).astype(x.dtype)


def chunk_mask(q_idx, k_idx):
    return (k_idx[None, :] // CHUNK) <= (q_idx[:, None] // CHUNK)


def rel_bucket(rel):
    nb = REL_BUCKETS // 2
    max_exact = nb // 2
    ret = jnp.where(rel > 0, nb, 0)
    n = jnp.abs(rel)
    nf = jnp.maximum(n, 1).astype(jnp.float32)
    large = max_exact + (jnp.log(nf / max_exact) / math.log(REL_MAX_DIST / max_exact) * (nb - max_exact)).astype(jnp.int32)
    large = jnp.minimum(large, nb - 1)
    return ret + jnp.where(n < max_exact, n, large)


def lambda_init(layer):
    return 0.8 - 0.6 * math.exp(-0.3 * layer)


def gmlp_branch(uv, ln_g, ln_b, w_s, b_s):
    b, s, _ = uv.shape
    uv = jax.nn.gelu(uv)
    u, v = uv[..., :GM_WIDTH], uv[..., GM_WIDTH:]
    v = layernorm(v, ln_g, ln_b).reshape(b, s // GM_SPAN, GM_SPAN, GM_GROUPS, GM_CH)
    p = jnp.arange(GM_SPAN)
    w = jnp.where(chunk_mask(p, p)[None], w_s, jnp.zeros_like(w_s))
    v = jnp.einsum('gpq,bnqgc->bnpgc', w, v) + b_s.T[None, None, :, :, None]
    return u * v.reshape(b, s, GM_WIDTH)


def mla_branch(cq, ckv, kr, g_cq, g_ckv, w_uq, w_ukv, cos, sin):
    b, s, _ = cq.shape
    q = (rmsnorm(cq, g_cq) @ w_uq).reshape(b, s, MLA_HEADS, MLA_NOPE + MLA_ROPE)
    q_nope = q[..., :MLA_NOPE].transpose(0, 2, 1, 3)
    q_rope = rope(q[..., MLA_NOPE:], cos[:, None], sin[:, None]).transpose(0, 2, 1, 3)
    kv = (rmsnorm(ckv, g_ckv) @ w_ukv).reshape(b, s, MLA_HEADS, MLA_NOPE + MLA_V)
    k_nope = kv[..., :MLA_NOPE].transpose(0, 2, 1, 3)
    v = kv[..., MLA_NOPE:].transpose(0, 2, 1, 3)
    k_rope = rope(kr, cos, sin)
    scale = (MLA_NOPE + MLA_ROPE) ** -0.5
    k_idx = jnp.arange(s)

    def block(blk):
        qs = blk * Q_BLOCK
        qn = lax.dynamic_slice_in_dim(q_nope, qs, Q_BLOCK, axis=2)
        qr = lax.dynamic_slice_in_dim(q_rope, qs, Q_BLOCK, axis=2)
        logits = (jnp.einsum('bhqd,bhkd->bhqk', qn, k_nope)
                  + jnp.einsum('bhqr,bkr->bhqk', qr, k_rope)).astype(jnp.float32) * scale
        mask = chunk_mask(qs + jnp.arange(Q_BLOCK), k_idx)
        p = jax.nn.softmax(jnp.where(mask, logits, NEG_INF), axis=-1)
        return jnp.einsum('bhqk,bhkd->bqhd', p.astype(v.dtype), v)

    out = lax.map(block, jnp.arange(s // Q_BLOCK))
    return out.transpose(1, 0, 2, 3, 4).reshape(b, s, MLA_HEADS * MLA_V)


def diff_branch(dq, dk, dv, lq1, lk1, lq2, lk2, g_sub, rel_bias, lam_init):
    b, s, _ = dq.shape
    q = dq.reshape(b, s, DIFF_HEADS, 2, DIFF_HEAD)
    k = dk.reshape(b, s, DIFF_HEADS, 2, DIFF_HEAD)
    q1 = q[..., 0, :].transpose(0, 2, 1, 3)
    q2 = q[..., 1, :].transpose(0, 2, 1, 3)
    k1 = k[..., 0, :].transpose(0, 2, 1, 3)
    k2 = k[..., 1, :].transpose(0, 2, 1, 3)
    v = dv.reshape(b, s, DIFF_HEADS, DIFF_V).transpose(0, 2, 1, 3)
    lam = (jnp.exp(jnp.sum(lq1.astype(jnp.float32) * lk1.astype(jnp.float32)))
           - jnp.exp(jnp.sum(lq2.astype(jnp.float32) * lk2.astype(jnp.float32))) + lam_init)
    scale = DIFF_HEAD ** -0.5
    k_idx = jnp.arange(s)

    def block(blk):
        qs = blk * Q_BLOCK
        q_idx = qs + jnp.arange(Q_BLOCK)
        q1b = lax.dynamic_slice_in_dim(q1, qs, Q_BLOCK, axis=2)
        q2b = lax.dynamic_slice_in_dim(q2, qs, Q_BLOCK, axis=2)
        bias = jnp.moveaxis(rel_bias[rel_bucket(k_idx[None, :] - q_idx[:, None])], -1, 0).astype(jnp.float32)
        mask = chunk_mask(q_idx, k_idx)
        s1 = jnp.einsum('bhqd,bhkd->bhqk', q1b, k1).astype(jnp.float32) * scale + bias
        s2 = jnp.einsum('bhqd,bhkd->bhqk', q2b, k2).astype(jnp.float32) * scale + bias
        a = (jax.nn.softmax(jnp.where(mask, s1, NEG_INF), axis=-1)
             - lam * jax.nn.softmax(jnp.where(mask, s2, NEG_INF), axis=-1))
        return jnp.einsum('bhqk,bhkd->bqhd', a.astype(v.dtype), v)

    out = lax.map(block, jnp.arange(s // Q_BLOCK))
    out = out.transpose(1, 0, 2, 3, 4).reshape(b, s, DIFF_HEADS, DIFF_V)
    out = rmsnorm(out, g_sub) * (1.0 - lam_init)
    return out.reshape(b, s, DIFF_HEADS * DIFF_V)


def swiglu(h, w1, w3, w2):
    return (jax.nn.silu(h @ w1) * (h @ w3)) @ w2


def moe_ffn(h, router, w1, w3, w2):
    logits = (h @ router).astype(jnp.float32)
    top_v, top_i = lax.top_k(logits, TOP_K)
    top_w = jax.nn.softmax(top_v, axis=-1)
    gates = jnp.sum(jax.nn.one_hot(top_i, N_EXPERTS, dtype=jnp.float32) * top_w[..., None], axis=-2)
    gates = gates.astype(h.dtype)
    y = gates[..., 0:1] * swiglu(h, w1[0], w3[0], w2[0])
    for e in range(1, N_EXPERTS):
        y = y + gates[..., e:e + 1] * swiglu(h, w1[e], w3[e], w2[e])
    return y


def setup_inputs(seed: int = 0) -> dict:
    key = jax.random.key(seed)
    ks = jax.random.split(key, 40)

    def nrm(i, shape, scale):
        return jax.random.normal(ks[i], shape, jnp.float32) * scale

    def gain(i, shape):
        return 1.0 + nrm(i, shape, 0.02)

    L = DEPTH
    D = D_MODEL
    return {
        'x': nrm(0, (BATCH, SEQ, D), 1.0),
        'g_mix': gain(1, (L, D)),
        'w_in': nrm(2, (L, D, IN_WIDTH), D ** -0.5),
        'gm_ln_g': gain(3, (L, GM_WIDTH)),
        'gm_ln_b': nrm(4, (L, GM_WIDTH), 0.02),
        'gm_w_s': nrm(5, (L, GM_GROUPS, GM_SPAN, GM_SPAN), GM_SPAN ** -0.5),
        'gm_b_s': gain(6, (L, GM_GROUPS, GM_SPAN)),
        'mla_g_cq': gain(7, (L, MLA_Q_RANK)),
        'mla_g_ckv': gain(8, (L, MLA_KV_RANK)),
        'mla_w_uq': nrm(9, (L, MLA_Q_RANK, MLA_HEADS * (MLA_NOPE + MLA_ROPE)), MLA_Q_RANK ** -0.5),
        'mla_w_ukv': nrm(10, (L, MLA_KV_RANK, MLA_HEADS * (MLA_NOPE + MLA_V)), MLA_KV_RANK ** -0.5),
        'diff_lq1': nrm(11, (L, DIFF_HEAD), 0.1),
        'diff_lk1': nrm(12, (L, DIFF_HEAD), 0.1),
        'diff_lq2': nrm(13, (L, DIFF_HEAD), 0.1),
        'diff_lk2': nrm(14, (L, DIFF_HEAD), 0.1),
        'diff_g_sub': gain(15, (L, DIFF_V)),
        'rel_bias': nrm(16, (REL_BUCKETS, DIFF_HEADS), 0.2),
        'w_gate': nrm(17, (L, N_BRANCH, D, D), D ** -0.5),
        'b_gate': nrm(18, (L, N_BRANCH, D), 0.02),
        'w_branch': nrm(19, (L, N_BRANCH, BRANCH_WIDTH, D), BRANCH_WIDTH ** -0.5),
        'w_o': nrm(20, (L, D, D), D ** -0.5),
        'g_ffn': gain(21, (L, D)),
        'ffn_w1': nrm(22, (N_DENSE, D, FFN_DENSE), D ** -0.5),
        'ffn_w3': nrm(23, (N_DENSE, D, FFN_DENSE), D ** -0.5),
        'ffn_w2': nrm(24, (N_DENSE, FFN_DENSE, D), FFN_DENSE ** -0.5),
        'moe_router': nrm(25, (N_MOE, D, N_EXPERTS), D ** -0.5),
        'moe_w1': nrm(26, (N_MOE, N_EXPERTS, D, FFN_EXPERT), D ** -0.5),
        'moe_w3': nrm(27, (N_MOE, N_EXPERTS, D, FFN_EXPERT), D ** -0.5),
        'moe_w2': nrm(28, (N_MOE, N_EXPERTS, FFN_EXPERT, D), FFN_EXPERT ** -0.5),
        'g_final': gain(29, (D,)),
    }


def reference(x, g_mix, w_in, gm_ln_g, gm_ln_b, gm_w_s, gm_b_s, mla_g_cq, mla_g_ckv, mla_w_uq, mla_w_ukv,
              diff_lq1, diff_lk1, diff_lq2, diff_lk2, diff_g_sub, rel_bias, w_gate, b_gate, w_branch, w_o,
              g_ffn, ffn_w1, ffn_w3, ffn_w2, moe_router, moe_w1, moe_w3, moe_w2, g_final):
    cos, sin = rope_tables(x.shape[1])
    for l in range(DEPTH):
        h = rmsnorm(x, g_mix[l])
        uv, cq, ckv, kr, dq, dk, dv = jnp.split(h @ w_in[l], IN_SPLITS, axis=-1)
        br_a = gmlp_branch(uv, gm_ln_g[l], gm_ln_b[l], gm_w_s[l], gm_b_s[l])
        br_b = mla_branch(cq, ckv, kr, mla_g_cq[l], mla_g_ckv[l], mla_w_uq[l], mla_w_ukv[l], cos, sin)
        br_c = diff_branch(dq, dk, dv, diff_lq1[l], diff_lk1[l], diff_lq2[l], diff_lk2[l], diff_g_sub[l],
                           rel_bias, lambda_init(l))
        merged = None
        for n, br in enumerate((br_a, br_b, br_c)):
            gate = jax.nn.sigmoid((h @ w_gate[l, n] + b_gate[l, n]).astype(jnp.float32)).astype(x.dtype)
            term = gate * (br @ w_branch[l, n])
            merged = term if merged is None else merged + term
        x = x + merged @ w_o[l]
        h = rmsnorm(x, g_ffn[l])
        if l % 2 == 0:
            i = l // 2
            x = x + swiglu(h, ffn_w1[i], ffn_w3[i], ffn_w2[i])
        else:
            i = l // 2
            x = x + moe_ffn(h, moe_router[i], moe_w1[i], moe_w3[i], moe_w2[i])
    return rmsnorm(x, g_final)
```

```python
import functools
import math

import jax
import jax.numpy as jnp
from jax import lax
from jax.experimental import pallas as pl
from jax.experimental.pallas import tpu as pltpu

F32 = jnp.float32
BF16 = jnp.bfloat16

EPS = 1e-6
NEG_INF = -1e30
CHUNK = 64

GM_WIDTH = 512
GM_GROUPS = 4
GM_CH = GM_WIDTH // GM_GROUPS
GM_SPAN = 128

MLA_HEADS = 4
MLA_Q_RANK = 256
MLA_KV_RANK = 128
MLA_NOPE = 128
MLA_ROPE = 64
MLA_V = 128
MLA_QK = MLA_NOPE + MLA_ROPE
ROPE_THETA = 10000.0

DIFF_HEADS = 4
DIFF_HEAD = 64
DIFF_V = 2 * DIFF_HEAD

REL_BUCKETS = 32
REL_MAX_DIST = 128

N_BRANCH = 3
BRANCH_WIDTH = 512
N_EXPERTS = 8

V7X_VMEM_LIMIT_BYTES = 56 * 1024 * 1024

_UV0, _UV1 = 0, 2 * GM_WIDTH
_CQ0, _CQ1 = _UV1, _UV1 + MLA_Q_RANK
_CKV0, _CKV1 = _CQ1, _CQ1 + MLA_KV_RANK
_KR0, _KR1 = _CKV1, _CKV1 + 2 * MLA_ROPE
_DQ0, _DQ1 = _KR1, _KR1 + 512
_DK0, _DK1 = _DQ1, _DQ1 + 512
_DV0, _DV1 = _DK1, _DK1 + 512
_W_ALL = _DV1


def _cparams(sem):
    return pltpu.CompilerParams(dimension_semantics=sem, vmem_limit_bytes=V7X_VMEM_LIMIT_BYTES)


def _const_spec(shape):
    zeros = (0,) * len(shape)
    return pl.BlockSpec(shape, lambda *_: zeros, pipeline_mode=pl.Buffered(1))


def _rms(x, g):
    return x * lax.rsqrt(jnp.mean(x * x, axis=-1, keepdims=True) + EPS) * g


def _gelu_tanh(x):
    c = math.sqrt(2.0 / math.pi)
    return x * (0.5 * (1.0 + jnp.tanh(c * (x + 0.044715 * (x * x * x)))))


def _sigmoid(x):
    return 1.0 / (1.0 + jnp.exp(-x))


def _dot(a, b):
    return jnp.dot(a, b, preferred_element_type=F32)


def _dot_nt(a, b):
    return lax.dot_general(a, b, (((1,), (1,)), ((), ())), preferred_element_type=F32)


def _mix_in_kernel(x_ref, g_ref, w_ref, lng_ref, lnb_ref, wsp_ref, bsp_ref, gcq_ref, gckv_ref,
                   wuq_ref, wukv_ref, rope_ref,
                   bra_ref, qm_ref, km_ref, vm_ref, dq_ref, dk_ref, dv_ref):
    hb = _rms(x_ref[...], g_ref[...]).astype(BF16)

    def proj(lo, hi):
        return _dot(hb, w_ref[:, lo:hi])

    uv = _gelu_tanh(proj(_UV0, _UV1))
    u = uv[:, :GM_WIDTH]
    v = uv[:, GM_WIDTH:]
    vc = v - jnp.mean(v, axis=-1, keepdims=True)
    vn = vc * lax.rsqrt(jnp.mean(vc * vc, axis=-1, keepdims=True) + EPS) * lng_ref[...] + lnb_ref[...]
    vb = vn.astype(BF16)
    for g in range(GM_GROUPS):
        cs = slice(g * GM_CH, (g + 1) * GM_CH)
        mixed = _dot(wsp_ref[g], vb[:, cs]) + bsp_ref[:, cs]
        bra_ref[:, cs] = (u[:, cs] * mixed).astype(BF16)

    rope_tab = rope_ref[...]

    def rotate(pair):
        z = pair * rope_tab
        return z + pltpu.roll(z, MLA_ROPE, axis=1)

    q_scale = MLA_QK ** -0.5
    cqn = _rms(proj(_CQ0, _CQ1), gcq_ref[...]).astype(BF16)
    for h in range(MLA_HEADS):
        qh = _dot(cqn, wuq_ref[:, h * 256:(h + 1) * 256])
        qm_ref[0, h, :, 0:MLA_NOPE] = (qh[:, :MLA_NOPE] * q_scale).astype(BF16)
        qr = rotate(qh[:, MLA_NOPE:])
        qm_ref[0, h, :, MLA_NOPE:MLA_QK] = (qr[:, :MLA_ROPE] * q_scale).astype(BF16)

    ckvn = _rms(proj(_CKV0, _CKV1), gckv_ref[...]).astype(BF16)
    kr = rotate(proj(_KR0, _KR1))[:, :MLA_ROPE].astype(BF16)
    for h in range(MLA_HEADS):
        kn = _dot(ckvn, wukv_ref[:, h * MLA_NOPE:(h + 1) * MLA_NOPE])
        km_ref[0, h, :, 0:MLA_NOPE] = kn.astype(BF16)
        km_ref[0, h, :, MLA_NOPE:MLA_QK] = kr
        vv = _dot(ckvn, wukv_ref[:, 512 + h * MLA_V:512 + (h + 1) * MLA_V])
        vm_ref[0, h] = vv.astype(BF16)

    dq_ref[...] = (proj(_DQ0, _DQ1) * (DIFF_HEAD ** -0.5)).astype(BF16)
    dk_ref[...] = proj(_DK0, _DK1).astype(BF16)
    dv_ref[...] = proj(_DV0, _DV1).astype(BF16)


def _mix_in(x2, lw, rope_tab, batch, seq, tm):
    t, d = x2.shape
    nt = t // tm
    ns = seq // tm
    tok = lambda i: (i, 0)
    head4 = lambda i: (i // ns, 0, i % ns, 0)
    out_shape = (
        jax.ShapeDtypeStruct((t, GM_WIDTH), BF16),
        jax.ShapeDtypeStruct((batch, MLA_HEADS, seq, MLA_QK), BF16),
        jax.ShapeDtypeStruct((batch, MLA_HEADS, seq, MLA_QK), BF16),
        jax.ShapeDtypeStruct((batch, MLA_HEADS, seq, MLA_V), BF16),
        jax.ShapeDtypeStruct((t, 512), BF16),
        jax.ShapeDtypeStruct((t, 512), BF16),
        jax.ShapeDtypeStruct((t, 512), BF16),
    )
    in_specs = [
        pl.BlockSpec((tm, d), tok),
        _const_spec((1, d)),
        _const_spec((d, _W_ALL)),
        _const_spec((1, GM_WIDTH)),
        _const_spec((1, GM_WIDTH)),
        _const_spec((GM_GROUPS, tm, tm)),
        _const_spec((tm, GM_WIDTH)),
        _const_spec((1, MLA_Q_RANK)),
        _const_spec((1, MLA_KV_RANK)),
        _const_spec((MLA_Q_RANK, 1024)),
        _const_spec((MLA_KV_RANK, 1024)),
        pl.BlockSpec((tm, 2 * MLA_ROPE), lambda i: (i % ns, 0)),
    ]
    out_specs = (
        pl.BlockSpec((tm, GM_WIDTH), tok),
        pl.BlockSpec((1, MLA_HEADS, tm, MLA_QK), head4),
        pl.BlockSpec((1, MLA_HEADS, tm, MLA_QK), head4),
        pl.BlockSpec((1, MLA_HEADS, tm, MLA_V), head4),
        pl.BlockSpec((tm, 512), tok),
        pl.BlockSpec((tm, 512), tok),
        pl.BlockSpec((tm, 512), tok),
    )
    return pl.pallas_call(
        _mix_in_kernel, out_shape=out_shape, grid=(nt,), in_specs=in_specs, out_specs=out_specs,
        compiler_params=_cparams(("parallel",)), name="mix_in",
    )(x2, lw["g_mix"], lw["w_all"], lw["ln_g"], lw["ln_b"], lw["w_sp"], lw["b_sp"], lw["g_cq"],
      lw["g_ckv"], lw["w_uq"], lw["w_ukv"], rope_tab)


def _chunk_mask(tq, tk):
    r = lax.broadcasted_iota(jnp.int32, (tq, tk), 0) // CHUNK
    c = lax.broadcasted_iota(jnp.int32, (tq, tk), 1) // CHUNK
    return c <= r


def _online_update(state, s, v):
    m, l, acc = state
    m_new = jnp.maximum(m, jnp.max(s, axis=-1, keepdims=True))
    p = jnp.exp(s - m_new)
    a = jnp.exp(m - m_new)
    l = a * l + jnp.sum(p, axis=-1, keepdims=True)
    acc = a * acc + _dot(p.astype(BF16), v)
    return m_new, l, acc


def _init_state(tq, dv):
    return (jnp.full((tq, 1), NEG_INF, F32), jnp.zeros((tq, 1), F32), jnp.zeros((tq, dv), F32))


def _mla_attn_kernel(q_ref, k_ref, v_ref, o_ref, *, tq):
    i = pl.program_id(2)
    q = q_ref[0, 0]

    def kv_tile(j):
        rows = pl.ds(pl.multiple_of(j * tq, tq), tq)
        return k_ref[0, 0, rows, :], v_ref[0, 0, rows, :]

    def past(j, state):
        k, v = kv_tile(j)
        return _online_update(state, _dot_nt(q, k), v)

    state = lax.fori_loop(0, i, past, _init_state(tq, MLA_V))
    k, v = kv_tile(i)
    s = jnp.where(_chunk_mask(tq, tq), _dot_nt(q, k), NEG_INF)
    _, l, acc = _online_update(state, s, v)
    o_ref[...] = (acc / l).astype(BF16)


def _mla_attention(qm, km, vm, tq):
    b, h, s, _ = qm.shape
    nq = s // tq
    return pl.pallas_call(
        functools.partial(_mla_attn_kernel, tq=tq),
        out_shape=jax.ShapeDtypeStruct((b * s, h * MLA_V), BF16),
        grid=(b, h, nq),
        in_specs=[
            pl.BlockSpec((1, 1, tq, MLA_QK), lambda bi, hi, i: (bi, hi, i, 0)),
            pl.BlockSpec((1, 1, s, MLA_QK), lambda bi, hi, i: (bi, hi, 0, 0)),
            pl.BlockSpec((1, 1, s, MLA_V), lambda bi, hi, i: (bi, hi, 0, 0)),
        ],
        out_specs=pl.BlockSpec((tq, MLA_V), lambda bi, hi, i: (bi * nq + i, hi)),
        compiler_params=_cparams(("parallel", "parallel", "arbitrary")), name="mla_attention",
    )(qm, km, vm)


def _diff_attn_kernel(q_ref, k_ref, v_ref, bias_ref, lam_ref, gsub_ref, o_ref, *, tq, lam_init):
    i = pl.program_id(2)
    q = q_ref[...]
    lane = lax.broadcasted_iota(jnp.int32, q.shape, 1)
    q1 = jnp.where(lane < DIFF_HEAD, q, jnp.zeros_like(q))
    q2 = jnp.where(lane >= DIFF_HEAD, q, jnp.zeros_like(q))

    def kv_tile(j):
        rows = pl.ds(pl.multiple_of(j * tq, tq), tq)
        return k_ref[rows, :], v_ref[rows, :]

    def step(j, states, bias, mask):
        st1, st2 = states
        k, v = kv_tile(j)
        s1 = _dot_nt(q1, k)
        s2 = _dot_nt(q2, k)
        if bias is not None:
            s1 = s1 + bias
            s2 = s2 + bias
        if mask is not None:
            s1 = jnp.where(mask, s1, NEG_INF)
            s2 = jnp.where(mask, s2, NEG_INF)
        return _online_update(st1, s1, v), _online_update(st2, s2, v)

    init = (_init_state(tq, DIFF_V), _init_state(tq, DIFF_V))
    states = lax.fori_loop(0, jnp.maximum(i - 1, 0), lambda j, st: step(j, st, None, None), init)
    states = lax.cond(i >= 1, lambda st: step(i - 1, st, bias_ref[0, 0], None), lambda st: st, states)
    (_, l1, acc1), (_, l2, acc2) = step(i, states, bias_ref[0, 1], _chunk_mask(tq, tq))

    lv = lam_ref[...]
    lam = (jnp.exp(jnp.sum(lv[0:1] * lv[1:2], axis=-1, keepdims=True))
           - jnp.exp(jnp.sum(lv[2:3] * lv[3:4], axis=-1, keepdims=True)) + lam_init)
    out = acc1 / l1 - lam * (acc2 / l2)
    o_ref[...] = (_rms(out, gsub_ref[...]) * (1.0 - lam_init)).astype(BF16)


def _diff_attention(dq, dk, dv, bias_near, lam_vec, g_sub, batch, seq, tq, lam_init):
    t = dq.shape[0]
    nq = seq // tq
    return pl.pallas_call(
        functools.partial(_diff_attn_kernel, tq=tq, lam_init=lam_init),
        out_shape=jax.ShapeDtypeStruct((t, DIFF_HEADS * DIFF_V), BF16),
        grid=(batch, DIFF_HEADS, nq),
        in_specs=[
            pl.BlockSpec((tq, 2 * DIFF_HEAD), lambda bi, hi, i: (bi * nq + i, hi)),
            pl.BlockSpec((seq, 2 * DIFF_HEAD), lambda bi, hi, i: (bi, hi)),
            pl.BlockSpec((seq, DIFF_V), lambda bi, hi, i: (bi, hi)),
            pl.BlockSpec((1, 2, tq, tq), lambda bi, hi, i: (hi, 0, 0, 0)),
            pl.BlockSpec((4, DIFF_HEAD), lambda bi, hi, i: (0, 0)),
            pl.BlockSpec((1, DIFF_V), lambda bi, hi, i: (0, 0)),
        ],
        out_specs=pl.BlockSpec((tq, DIFF_V), lambda bi, hi, i: (bi * nq + i, hi)),
        compiler_params=_cparams(("parallel", "parallel", "arbitrary")), name="diff_attention",
    )(dq, dk, dv, bias_near, lam_vec, g_sub)


def _merge_kernel(*refs, with_router):
    if with_router:
        (x_ref, gmix_ref, bra_ref, brb_ref, brc_ref, wg_ref, bg_ref, wb_ref, wo_ref, gffn_ref, router_ref,
         xo_ref, h2_ref, gates_ref) = refs
    else:
        (x_ref, gmix_ref, bra_ref, brb_ref, brc_ref, wg_ref, bg_ref, wb_ref, wo_ref, gffn_ref,
         xo_ref, h2_ref) = refs
    x = x_ref[...]
    hb = _rms(x, gmix_ref[...]).astype(BF16)
    merged = None
    for n, br_ref in enumerate((bra_ref, brb_ref, brc_ref)):
        gate = _sigmoid(_dot(hb, wg_ref[n]) + bg_ref[n:n + 1, :])
        term = gate * _dot(br_ref[...], wb_ref[n])
        merged = term if merged is None else merged + term
    x_new = x + _dot(merged.astype(BF16), wo_ref[...])
    xo_ref[...] = x_new
    h2 = _rms(x_new, gffn_ref[...])
    h2_ref[...] = h2.astype(BF16)
    if with_router:
        logits = jnp.dot(h2, router_ref[...], preferred_element_type=F32, precision=lax.Precision.HIGHEST)
        idx = lax.broadcasted_iota(jnp.int32, logits.shape, 1)
        v1 = jnp.max(logits, axis=-1, keepdims=True)
        i1 = jnp.min(jnp.where(logits == v1, idx, N_EXPERTS), axis=-1, keepdims=True)
        rest = jnp.where(idx == i1, -jnp.inf, logits)
        v2 = jnp.max(rest, axis=-1, keepdims=True)
        i2 = jnp.min(jnp.where(rest == v2, idx, N_EXPERTS), axis=-1, keepdims=True)
        e = jnp.exp(v2 - v1)
        w1 = 1.0 / (1.0 + e)
        w2 = e / (1.0 + e)
        gates_ref[...] = jnp.where(idx == i1, w1, 0.0) + jnp.where(idx == i2, w2, 0.0)


def _merge(x2, br_a, br_b, br_c, lw, tm, router):
    t, d = x2.shape
    tok = lambda i: (i, 0)
    with_router = router is not None
    in_specs = [
        pl.BlockSpec((tm, d), tok),
        _const_spec((1, d)),
        pl.BlockSpec((tm, BRANCH_WIDTH), tok),
        pl.BlockSpec((tm, BRANCH_WIDTH), tok),
        pl.BlockSpec((tm, BRANCH_WIDTH), tok),
        _const_spec((N_BRANCH, d, d)),
        _const_spec((N_BRANCH, d)),
        _const_spec((N_BRANCH, BRANCH_WIDTH, d)),
        _const_spec((d, d)),
        _const_spec((1, d)),
    ]
    args = [x2, lw["g_mix"], br_a, br_b, br_c, lw["w_gate"], lw["b_gate"], lw["w_branch"], lw["w_o"], lw["g_ffn"]]
    out_shape = [jax.ShapeDtypeStruct((t, d), F32), jax.ShapeDtypeStruct((t, d), BF16)]
    out_specs = [pl.BlockSpec((tm, d), tok), pl.BlockSpec((tm, d), tok)]
    if with_router:
        in_specs.append(_const_spec((d, N_EXPERTS)))
        args.append(router)
        out_shape.append(jax.ShapeDtypeStruct((t, N_EXPERTS), F32))
        out_specs.append(pl.BlockSpec((tm, N_EXPERTS), tok))
    return pl.pallas_call(
        functools.partial(_merge_kernel, with_router=with_router),
        out_shape=tuple(out_shape), grid=(t // tm,), in_specs=in_specs, out_specs=tuple(out_specs),
        compiler_params=_cparams(("parallel",)), name="merge",
    )(*args)


def _swiglu_chunk(hb, w1, w3, w2):
    a = _dot(hb, w1)
    b = _dot(hb, w3)
    return _dot((a * _sigmoid(a) * b).astype(BF16), w2)


def _dense_ffn_kernel(x_ref, h_ref, w1_ref, w3_ref, w2_ref, o_ref, *, fc):
    hb = h_ref[...]
    f = w1_ref.shape[1]
    acc = x_ref[...]
    for c in range(f // fc):
        cols = slice(c * fc, (c + 1) * fc)
        acc = acc + _swiglu_chunk(hb, w1_ref[:, cols], w3_ref[:, cols], w2_ref[cols, :])
    o_ref[...] = acc


def _dense_ffn(x2, h2, w1, w3, w2, tm, fc):
    t, d = x2.shape
    f = w1.shape[1]
    tok = lambda i: (i, 0)
    return pl.pallas_call(
        functools.partial(_dense_ffn_kernel, fc=fc),
        out_shape=jax.ShapeDtypeStruct((t, d), F32), grid=(t // tm,),
        in_specs=[pl.BlockSpec((tm, d), tok), pl.BlockSpec((tm, d), tok),
                  _const_spec((d, f)), _const_spec((d, f)), _const_spec((f, d))],
        out_specs=pl.BlockSpec((tm, d), tok),
        compiler_params=_cparams(("parallel",)), name="dense_ffn",
    )(x2, h2, w1, w3, w2)


def _moe_dense_kernel(x_ref, h_ref, gates_ref, w1_ref, w3_ref, w2_ref, o_ref):
    e = pl.program_id(1)
    c = pl.program_id(2)

    @pl.when(jnp.logical_and(e == 0, c == 0))
    def _():
        o_ref[...] = x_ref[...]

    gates = gates_ref[...]
    idx = lax.broadcasted_iota(jnp.int32, gates.shape, 1)
    gate = jnp.sum(jnp.where(idx == e, gates, 0.0), axis=-1, keepdims=True)
    o_ref[...] += gate * _swiglu_chunk(h_ref[...], w1_ref[0], w3_ref[0], w2_ref[0])


def _moe_dense(x2, h2, gates, w1, w3, w2, tm, fc):
    t, d = x2.shape
    ne, _, f = w1.shape
    tok = lambda i, e, c: (i, 0)
    return pl.pallas_call(
        _moe_dense_kernel,
        out_shape=jax.ShapeDtypeStruct((t, d), F32), grid=(t // tm, ne, f // fc),
        in_specs=[pl.BlockSpec((tm, d), tok), pl.BlockSpec((tm, d), tok), pl.BlockSpec((tm, N_EXPERTS), tok),
                  pl.BlockSpec((1, d, fc), lambda i, e, c: (e, 0, c)),
                  pl.BlockSpec((1, d, fc), lambda i, e, c: (e, 0, c)),
                  pl.BlockSpec((1, fc, d), lambda i, e, c: (e, c, 0))],
        out_specs=pl.BlockSpec((tm, d), tok),
        compiler_params=_cparams(("parallel", "arbitrary", "arbitrary")), name="moe_dense",
    )(x2, h2, gates, w1, w3, w2)


def _final_norm_kernel(x_ref, g_ref, o_ref):
    o_ref[...] = _rms(x_ref[...], g_ref[...])


def _final_norm(x2, g, tm):
    t, d = x2.shape
    return pl.pallas_call(
        _final_norm_kernel, out_shape=jax.ShapeDtypeStruct((t, d), F32), grid=(t // tm,),
        in_specs=[pl.BlockSpec((tm, d), lambda i: (i, 0)), _const_spec((1, d))],
        out_specs=pl.BlockSpec((tm, d), lambda i: (i, 0)),
        compiler_params=_cparams(("parallel",)), name="final_norm",
    )(x2, g)


def _swap_halves_cols(w):
    half = w.shape[-1] // 2
    return jnp.concatenate([w[..., half:], w[..., :half]], axis=-1)


def _rope_table(seq):
    pos = jnp.arange(seq, dtype=F32)
    inv = ROPE_THETA ** (-jnp.arange(0, MLA_ROPE, 2, dtype=F32) / MLA_ROPE)
    ang = pos[:, None] * inv[None, :]
    cos, sin = jnp.cos(ang), jnp.sin(ang)
    return jnp.concatenate([cos, cos, -sin, sin], axis=-1)


def _rel_bucket(rel):
    nb = REL_BUCKETS // 2
    max_exact = nb // 2
    ret = jnp.where(rel > 0, nb, 0)
    n = jnp.abs(rel)
    nf = jnp.maximum(n, 1).astype(F32)
    large = max_exact + (jnp.log(nf / max_exact) / math.log(REL_MAX_DIST / max_exact) * (nb - max_exact)).astype(jnp.int32)
    large = jnp.minimum(large, nb - 1)
    return ret + jnp.where(n < max_exact, n, large)


def _near_bias(rel_bias, tq):
    qi = jnp.arange(tq)[:, None]
    ki = jnp.arange(tq)[None, :]
    rel = jnp.stack([ki - qi - tq, ki - qi])
    bias = jnp.moveaxis(rel_bias[_rel_bucket(rel)], -1, 0).astype(F32)
    far = rel_bias[_rel_bucket(jnp.int32(-2 * tq))].astype(F32)
    return bias - far[:, None, None, None]


def _layer_weights(p, l, tm):
    d = p["w_in"].shape[1]
    w_in = p["w_in"][l]
    sizes = (2 * GM_WIDTH, MLA_Q_RANK, MLA_KV_RANK, MLA_ROPE, 512, 512, 512)
    offs = [0]
    for s in sizes:
        offs.append(offs[-1] + s)
    seg = [w_in[:, offs[k]:offs[k + 1]] for k in range(len(sizes))]
    w_all = jnp.concatenate([seg[0], seg[1], seg[2], seg[3], _swap_halves_cols(seg[3]), seg[4], seg[5], seg[6]],
                            axis=1).astype(BF16)
    pidx = jnp.arange(GM_SPAN)
    mask = (pidx[None, :] // CHUNK) <= (pidx[:, None] // CHUNK)
    w_s = jnp.where(mask[None], p["gm_w_s"][l], jnp.zeros_like(p["gm_w_s"][l]))
    nsp = tm // GM_SPAN
    w_sp = jnp.einsum("ab,gpq->gapbq", jnp.eye(nsp, dtype=F32), w_s).reshape(GM_GROUPS, tm, tm).astype(BF16)
    b_sp = jnp.tile(jnp.repeat(p["gm_b_s"][l].T, GM_CH, axis=1), (nsp, 1))
    w_uq = p["mla_w_uq"][l].reshape(MLA_Q_RANK, MLA_HEADS, MLA_QK)
    w_uq = jnp.concatenate([w_uq, _swap_halves_cols(w_uq[..., MLA_NOPE:])], axis=-1).reshape(MLA_Q_RANK, -1)
    w_ukv = p["mla_w_ukv"][l].reshape(MLA_KV_RANK, MLA_HEADS, MLA_NOPE + MLA_V)
    w_ukv = jnp.concatenate([w_ukv[..., :MLA_NOPE].reshape(MLA_KV_RANK, -1),
                             w_ukv[..., MLA_NOPE:].reshape(MLA_KV_RANK, -1)], axis=1)
    return {
        "g_mix": p["g_mix"][l][None], "w_all": w_all,
        "ln_g": p["gm_ln_g"][l][None], "ln_b": p["gm_ln_b"][l][None], "w_sp": w_sp, "b_sp": b_sp,
        "g_cq": p["mla_g_cq"][l][None], "g_ckv": p["mla_g_ckv"][l][None],
        "w_uq": w_uq.astype(BF16), "w_ukv": w_ukv.astype(BF16),
        "lam_vec": jnp.stack([p["diff_lq1"][l], p["diff_lk1"][l], p["diff_lq2"][l], p["diff_lk2"][l]]),
        "g_sub": p["diff_g_sub"][l][None],
        "w_gate": p["w_gate"][l].astype(BF16), "b_gate": p["b_gate"][l],
        "w_branch": p["w_branch"][l].astype(BF16), "w_o": p["w_o"][l].astype(BF16),
        "g_ffn": p["g_ffn"][l][None],
    }


def _lambda_init(layer):
    return 0.8 - 0.6 * math.exp(-0.3 * layer)


def kernel(x, g_mix, w_in, gm_ln_g, gm_ln_b, gm_w_s, gm_b_s, mla_g_cq, mla_g_ckv, mla_w_uq, mla_w_ukv,
           diff_lq1, diff_lk1, diff_lq2, diff_lk2, diff_g_sub, rel_bias, w_gate, b_gate, w_branch, w_o,
           g_ffn, ffn_w1, ffn_w3, ffn_w2, moe_router, moe_w1, moe_w3, moe_w2, g_final):
    p = dict(g_mix=g_mix, w_in=w_in, gm_ln_g=gm_ln_g, gm_ln_b=gm_ln_b, gm_w_s=gm_w_s, gm_b_s=gm_b_s,
             mla_g_cq=mla_g_cq, mla_g_ckv=mla_g_ckv, mla_w_uq=mla_w_uq, mla_w_ukv=mla_w_ukv,
             diff_lq1=diff_lq1, diff_lk1=diff_lk1, diff_lq2=diff_lq2, diff_lk2=diff_lk2,
             diff_g_sub=diff_g_sub, w_gate=w_gate, b_gate=b_gate, w_branch=w_branch, w_o=w_o, g_ffn=g_ffn)
    batch, seq, d = x.shape
    depth = w_in.shape[0]
    tm = min(512, seq)
    tq = min(512, seq)
    rope_tab = _rope_table(seq)
    bias_near = _near_bias(rel_bias, tq)
    x2 = x.reshape(batch * seq, d)
    for l in range(depth):
        lw = _layer_weights(p, l, tm)
        br_a, qm, km, vm, dq, dk, dv = _mix_in(x2, lw, rope_tab, batch, seq, tm)
        br_b = _mla_attention(qm, km, vm, tq)
        br_c = _diff_attention(dq, dk, dv, bias_near, lw["lam_vec"], lw["g_sub"], batch, seq, tq, _lambda_init(l))
        i = l // 2
        if l % 2 == 0:
            x2, h2 = _merge(x2, br_a, br_b, br_c, lw, tm, None)
            f = ffn_w1.shape[2]
            fc = 256 if f % 256 == 0 else 128
            x2 = _dense_ffn(x2, h2, ffn_w1[i].astype(BF16), ffn_w3[i].astype(BF16), ffn_w2[i].astype(BF16), tm, fc)
        else:
            x2, h2, gates = _merge(x2, br_a, br_b, br_c, lw, tm, moe_router[i])
            f = moe_w1.shape[3]
            fc = 512 if f % 512 == 0 else 128
            x2 = _moe_dense(x2, h2, gates, moe_w1[i].astype(BF16), moe_w3[i].astype(BF16),
                            moe_w2[i].astype(BF16), tm, fc)
    return _final_norm(x2, g_final[None], tm).reshape(batch, seq, d)
```
